```python
import jax, jax.numpy as jnp
from jax import lax
import numpy as np

D_MODEL = 4096
BATCH = 4
SEQ = 4096
DEPTH = 1

D_POOL = D_MODEL // 2
POOL_WINDOWS = (2, 4, 8, 16)
N_POOL_GROUPS = len(POOL_WINDOWS)
POOL_GROUP = D_POOL // N_POOL_GROUPS
D_SGU = D_MODEL // 2
SGU_CHUNK = 128
SGU_HEADS = 16
SGU_HEAD_DIM = D_SGU // SGU_HEADS
N_BRANCH = 2
D_IN = D_POOL + 2 * D_SGU + N_BRANCH * D_MODEL
SPLITS = (D_POOL, D_POOL + D_SGU, D_POOL + 2 * D_SGU, D_POOL + 2 * D_SGU + D_MODEL)
N_EXPERTS = 64
TOP_K = 8
N_GROUPS = 8
TOPK_GROUPS = 4
D_EXPERT = 512
D_SHARED = 512
ROUTED_SCALE = 2.5
MOE_BLOCK = 128
N_MOD = 6
EPS = 1e-6

kernel_name = "gated_pool_sgu_moe_block"


def _rmsnorm(x):
    xf = x.astype(jnp.float32)
    y = xf * lax.rsqrt(jnp.mean(xf * xf, axis=-1, keepdims=True) + EPS)
    return y.astype(x.dtype)


def _modulate(h, shift, scale):
    return h * (1.0 + scale[:, None, :]) + shift[:, None, :]


def _pool_mixer(a, w_pool, b_pool, pool_scale):
    bsz, seq, _ = a.shape
    af = a.astype(jnp.float32).reshape(bsz, seq, N_POOL_GROUPS, POOL_GROUP)
    csum = jnp.cumsum(af, axis=1)
    pos = jnp.arange(seq)
    pooled = []
    for g, w in enumerate(POOL_WINDOWS):
        cs = csum[:, :, g, :]
        lagged = jnp.pad(cs, ((0, 0), (w, 0), (0, 0)))[:, :seq, :]
        count = jnp.minimum(pos + 1, w).astype(jnp.float32)[None, :, None]
        pooled.append((cs - lagged) / count)
    pooled = jnp.stack(pooled, axis=2)
    diff = (pooled - af).astype(a.dtype)
    y = jnp.einsum('bsgc,gcd->bsgd', diff, w_pool) + b_pool
    return y.reshape(bsz, seq, D_POOL) * pool_scale


def _sgu_mixer(u, v, ln_g, ln_b, w_spatial, b_spatial):
    bsz, seq, _ = u.shape
    vf = v.astype(jnp.float32)
    mu = jnp.mean(vf, axis=-1, keepdims=True)
    var = jnp.mean(jnp.square(vf - mu), axis=-1, keepdims=True)
    vn = ((vf - mu) * lax.rsqrt(var + EPS)).astype(v.dtype) * ln_g + ln_b
    n_chunks = seq // SGU_CHUNK
    vc = vn.reshape(bsz, n_chunks, SGU_CHUNK, SGU_HEADS, SGU_HEAD_DIM)
    causal = jnp.tril(jnp.ones((SGU_CHUNK, SGU_CHUNK), dtype=bool))
    ws = jnp.where(causal[None], w_spatial, jnp.zeros_like(w_spatial))
    z = jnp.einsum('hts,bnshd->bnthd', ws, vc) + b_spatial.T[None, None, :, :, None]
    return u * z.reshape(bsz, seq, D_SGU)


def _mixer_block(h, w_in, w_pool, b_pool, pool_scale, sgu_ln_g, sgu_ln_b,
                 w_spatial, b_spatial, w_branch_pool, w_branch_sgu, w_out):
    proj = h @ w_in
    a, u, v, ga, gb = jnp.split(proj, SPLITS, axis=-1)
    y_a = _pool_mixer(a, w_pool, b_pool, pool_scale) @ w_branch_pool
    y_b = _sgu_mixer(u, v, sgu_ln_g, sgu_ln_b, w_spatial, b_spatial) @ w_branch_sgu
    merged = jax.nn.sigmoid(ga) * y_a + jax.nn.sigmoid(gb) * y_b
    return merged @ w_out


def _route(xt, w_router, router_bias):
    n_tok = xt.shape[0]
    scores = jax.nn.sigmoid((xt @ w_router).astype(jnp.float32))
    biased = scores + router_bias.astype(jnp.float32)
    grouped = biased.reshape(n_tok, N_GROUPS, N_EXPERTS // N_GROUPS)
    group_score = jnp.sum(lax.top_k(grouped, 2)[0], axis=-1)
    _, top_groups = lax.top_k(group_score, TOPK_GROUPS)
    group_mask = jnp.any(top_groups[:, :, None] == jnp.arange(N_GROUPS)[None, None, :], axis=1)
    expert_mask = jnp.repeat(group_mask, N_EXPERTS // N_GROUPS, axis=1)
    _, top_e = lax.top_k(jnp.where(expert_mask, biased, -jnp.inf), TOP_K)
    wts = jnp.take_along_axis(scores, top_e, axis=1)
    wts = wts / jnp.sum(wts, axis=-1, keepdims=True) * ROUTED_SCALE
    return top_e, wts


def _swiglu(x, w_gate, w_up, w_down):
    return (jax.nn.silu(x @ w_gate) * (x @ w_up)) @ w_down


def _moe(h, w_router, router_bias, w_exp_gate, w_exp_up, w_exp_down,
         w_sh_gate, w_sh_up, w_sh_down):
    bsz, seq, d = h.shape
    n_tok = bsz * seq
    xt = h.reshape(n_tok, d)
    top_e, wts = _route(xt, w_router, router_bias)
    n_pairs = n_tok * TOP_K
    e_flat = top_e.reshape(n_pairs)
    w_flat = wts.reshape(n_pairs)
    tok_flat = jnp.arange(n_pairs, dtype=jnp.int32) // TOP_K
    order = jnp.argsort(e_flat)
    e_sorted = e_flat[order]
    counts = jnp.zeros((N_EXPERTS,), jnp.int32).at[e_flat].add(1)
    starts = jnp.cumsum(counts) - counts
    padded = (counts + MOE_BLOCK - 1) // MOE_BLOCK * MOE_BLOCK
    pends = jnp.cumsum(padded)
    pstarts = pends - padded
    dest = pstarts[e_sorted] + (jnp.arange(n_pairs, dtype=jnp.int32) - starts[e_sorted])
    n_blocks = -(-n_pairs // MOE_BLOCK) + N_EXPERTS
    n_rows = n_blocks * MOE_BLOCK
    row_tok = jnp.full((n_rows,), n_tok, jnp.int32).at[dest].set(tok_flat[order])
    row_w = jnp.zeros((n_rows,), jnp.float32).at[dest].set(w_flat[order])
    block_start = jnp.arange(n_blocks, dtype=jnp.int32) * MOE_BLOCK
    block_e = jnp.minimum(jnp.searchsorted(pends, block_start, side='right'), N_EXPERTS - 1)
    xt_pad = jnp.concatenate([xt, jnp.zeros((1, d), xt.dtype)], axis=0)

    def step(acc, blk):
        toks, bw, e = blk
        xb = xt_pad[toks]
        yb = _swiglu(xb, w_exp_gate[e], w_exp_up[e], w_exp_down[e])
        return acc.at[toks].add(yb.astype(jnp.float32) * bw[:, None]), None

    acc0 = jnp.zeros((n_tok + 1, d), jnp.float32)
    acc, _ = lax.scan(step, acc0, (row_tok.reshape(n_blocks, MOE_BLOCK),
                                   row_w.reshape(n_blocks, MOE_BLOCK), block_e))
    routed = acc[:n_tok].astype(h.dtype)
    shared = _swiglu(xt, w_sh_gate, w_sh_up, w_sh_down)
    return (routed + shared).reshape(bsz, seq, d)


def setup_inputs(seed: int = 0) -> dict:
    key = jax.random.key(seed)
    ks = jax.random.split(key, 24)
    L, D = DEPTH, D_MODEL

    def nrm(k, shape, scale):
        return jax.random.normal(k, shape, jnp.float32) * scale

    return {
        "x": nrm(ks[0], (BATCH, SEQ, D), 1.0),
        "c": nrm(ks[1], (BATCH, D), 1.0),
        "w_ada": nrm(ks[2], (L, D, N_MOD * D), 0.5 * D ** -0.5),
        "b_ada": nrm(ks[3], (L, N_MOD * D), 0.02),
        "w_in": nrm(ks[4], (L, D, D_IN), D ** -0.5),
        "w_pool": nrm(ks[5], (L, N_POOL_GROUPS, POOL_GROUP, POOL_GROUP), POOL_GROUP ** -0.5),
        "b_pool": nrm(ks[6], (L, N_POOL_GROUPS, POOL_GROUP), 0.02),
        "pool_scale": 1.0 + nrm(ks[7], (L, D_POOL), 0.1),
        "sgu_ln_g": 1.0 + nrm(ks[8], (L, D_SGU), 0.1),
        "sgu_ln_b": nrm(ks[9], (L, D_SGU), 0.02),
        "w_spatial": nrm(ks[10], (L, SGU_HEADS, SGU_CHUNK, SGU_CHUNK), SGU_CHUNK ** -0.5),
        "b_spatial": 1.0 + nrm(ks[11], (L, SGU_HEADS, SGU_CHUNK), 0.1),
        "w_branch_pool": nrm(ks[12], (L, D_POOL, D), D_POOL ** -0.5),
        "w_branch_sgu": nrm(ks[13], (L, D_SGU, D), D_SGU ** -0.5),
        "w_out": nrm(ks[14], (L, D, D), D ** -0.5),
        "w_router": nrm(ks[15], (L, D, N_EXPERTS), D ** -0.5),
        "router_bias": nrm(ks[16], (L, N_EXPERTS), 0.01),
        "w_exp_gate": nrm(ks[17], (L, N_EXPERTS, D, D_EXPERT), D ** -0.5),
        "w_exp_up": nrm(ks[18], (L, N_EXPERTS, D, D_EXPERT), D ** -0.5),
        "w_exp_down": nrm(ks[19], (L, N_EXPERTS, D_EXPERT, D), D_EXPERT ** -0.5),
        "w_sh_gate": nrm(ks[20], (L, D, D_SHARED), D ** -0.5),
        "w_sh_up": nrm(ks[21], (L, D, D_SHARED), D ** -0.5),
        "w_sh_down": nrm(ks[22], (L, D_SHARED, D), D_SHARED ** -0.5),
        "final_gain": 1.0 + nrm(ks[23], (D,), 0.1),
    }


def reference(x, c, w_ada, b_ada, w_in, w_pool, b_pool, pool_scale, sgu_ln_g, sgu_ln_b,
              w_spatial, b_spatial, w_branch_pool, w_branch_sgu, w_out, w_router,
              router_bias, w_exp_gate, w_exp_up, w_exp_down, w_sh_gate, w_sh_up,
              w_sh_down, final_gain):
    cond = jax.nn.silu(c)
    for l in range(DEPTH):
        mod = cond @ w_ada[l] + b_ada[l]
        sh1, sc1, g1, sh2, sc2, g2 = jnp.split(mod, N_MOD, axis=-1)
        h = _modulate(_rmsnorm(x), sh1, sc1)
        mix = _mixer_block(h, w_in[l], w_pool[l], b_pool[l], pool_scale[l], sgu_ln_g[l],
                           sgu_ln_b[l], w_spatial[l], b_spatial[l], w_branch_pool[l],
                           w_branch_sgu[l], w_out[l])
        x = x + g1[:, None, :] * mix
        h = _modulate(_rmsnorm(x), sh2, sc2)
        ffn = _moe(h, w_router[l], router_bias[l], w_exp_gate[l], w_exp_up[l],
                   w_exp_down[l], w_sh_gate[l], w_sh_up[l], w_sh_down[l])
        x = x + g2[:, None, :] * ffn
    return _rmsnorm(x) * final_gain
```

```python
import functools

import jax
import jax.numpy as jnp
from jax import lax
from jax.experimental import pallas as pl
from jax.experimental.pallas import tpu as pltpu

D_MODEL = 4096
BATCH = 4
SEQ = 4096
N_TOK = BATCH * SEQ
D_POOL = D_MODEL // 2
POOL_WINDOWS = (2, 4, 8, 16)
POOL_GROUP = D_POOL // len(POOL_WINDOWS)
POOL_HALO = 16
D_SGU = D_MODEL // 2
SGU_CHUNK = 128
SGU_HEADS = 16
SGU_HEAD_DIM = D_SGU // SGU_HEADS
D_IN = D_POOL + 2 * D_SGU + 2 * D_MODEL
N_EXPERTS = 64
TOP_K = 8
N_GROUPS = 8
GROUP_SIZE = N_EXPERTS // N_GROUPS
TOPK_GROUPS = 4
D_EXPERT = 512
D_SHARED = 512
ROUTED_SCALE = 2.5
N_MOD = 6
EPS = 1e-6
HALF = D_MODEL // 2

ROW_BLOCK = 256
N_PAIRS = N_TOK * TOP_K
N_BLOCKS = N_PAIRS // ROW_BLOCK + N_EXPERTS
N_ROWS = N_BLOCKS * ROW_BLOCK

V7X_VMEM_LIMIT = 56 * 1024 * 1024

BF16 = jnp.bfloat16
F32 = jnp.float32
HI_MASK = 0xFFFF0000


def _params(sem, limit=V7X_VMEM_LIMIT):
    return pltpu.CompilerParams(dimension_semantics=sem, vmem_limit_bytes=limit)


def _pack_pair(lo_f32, hi_f32):
    lo = lax.bitcast_convert_type(lo_f32.astype(BF16).astype(F32), jnp.uint32)
    hi = lax.bitcast_convert_type(hi_f32.astype(BF16).astype(F32), jnp.uint32)
    return (lo >> 16) | (hi & jnp.uint32(HI_MASK))


def _unpack_pair(p):
    lo = lax.bitcast_convert_type(p << 16, F32)
    hi = lax.bitcast_convert_type(p & jnp.uint32(HI_MASK), F32)
    return lo, hi


ADA_TN = 1024


def _ada_kernel(c_ref, w_ref, b_ref, o_ref):
    c = c_ref[...]
    s = (c * jax.nn.sigmoid(c)).astype(BF16)
    o_ref[...] = jnp.dot(s, w_ref[...].astype(BF16), preferred_element_type=F32) + b_ref[...]


def _ada(c8, w_ada, b_ada):
    n = N_MOD * D_MODEL
    return pl.pallas_call(
        _ada_kernel,
        grid=(n // ADA_TN,),
        in_specs=[
            pl.BlockSpec((8, D_MODEL), lambda j: (0, 0)),
            pl.BlockSpec((D_MODEL, ADA_TN), lambda j: (0, j)),
            pl.BlockSpec((1, ADA_TN), lambda j: (0, j)),
        ],
        out_specs=pl.BlockSpec((8, ADA_TN), lambda j: (0, j)),
        out_shape=jax.ShapeDtypeStruct((8, n), F32),
        compiler_params=_params(("arbitrary",)),
        name="ada",
    )(c8, w_ada, b_ada)


PROJ_TM = 512
PROJ_TN = 1024
NORM_ROWS = 64


def _proj_kernel(x_ref, sh_ref, sc_ref, w_ref, o_ref, h_ref):
    i = pl.program_id(0)
    j = pl.program_id(1)

    @pl.when(j == 0)
    def _():
        b = (i * PROJ_TM) // SEQ
        sc = 1.0 + sc_ref[0, pl.ds(b, 1), :]
        sh = sh_ref[0, pl.ds(b, 1), :]

        def body(r, carry):
            rows = pl.ds(pl.multiple_of(r * NORM_ROWS, NORM_ROWS), NORM_ROWS)
            x = x_ref[rows, :]
            ms = jnp.mean(x * x, axis=-1, keepdims=True)
            h_ref[rows, :] = (x * lax.rsqrt(ms + EPS) * sc + sh).astype(BF16)
            return carry

        lax.fori_loop(0, PROJ_TM // NORM_ROWS, body, 0)

    o_ref[...] = jnp.dot(h_ref[...], w_ref[...], preferred_element_type=F32).astype(o_ref.dtype)


def _proj(x2, mod_r, w_in_bf):
    return pl.pallas_call(
        _proj_kernel,
        grid=(N_TOK // PROJ_TM, D_IN // PROJ_TN),
        in_specs=[
            pl.BlockSpec((PROJ_TM, D_MODEL), lambda i, j: (i, 0)),
            pl.BlockSpec((1, 8, D_MODEL), lambda i, j: (0, 0, 0)),
            pl.BlockSpec((1, 8, D_MODEL), lambda i, j: (1, 0, 0)),
            pl.BlockSpec((D_MODEL, PROJ_TN), lambda i, j: (0, j)),
        ],
        out_specs=pl.BlockSpec((PROJ_TM, PROJ_TN), lambda i, j: (i, j)),
        out_shape=jax.ShapeDtypeStruct((N_TOK, D_IN), BF16),
        scratch_shapes=[pltpu.VMEM((PROJ_TM, D_MODEL), BF16)],
        compiler_params=_params(("arbitrary", "arbitrary")),
        name="proj",
    )(x2, mod_r, mod_r, w_in_bf)


MIX_TM = 512
MIX_TN = 1024
COL_U = D_POOL // D_SGU
COL_V = COL_U + 1
COL_GA = (D_POOL + 2 * D_SGU) // MIX_TN
COL_GB = COL_GA + D_MODEL // MIX_TN


def _mix_kernel(a_ref, halo_ref, u_ref, v_ref, ga_ref, gb_ref, wpool_ref, bpool_ref, pscale_ref,
                lng_ref, lnb_ref, ws_ref, bst_ref, wbp_ref, wbs_ref, o_ref, pa_ref, ub_ref, vn_ref):
    i = pl.program_id(0)
    n = pl.program_id(1)

    @pl.when(n == 0)
    def _():
        pos0 = (i * MIX_TM) % SEQ
        r = lax.broadcasted_iota(jnp.int32, (MIX_TM, MIX_TM), 0)
        c = lax.broadcasted_iota(jnp.int32, (MIX_TM, MIX_TM), 1)
        d_main = r - c
        rh = lax.broadcasted_iota(jnp.int32, (MIX_TM, POOL_HALO), 0)
        ch = lax.broadcasted_iota(jnp.int32, (MIX_TM, POOL_HALO), 1)
        d_halo = rh + POOL_HALO - ch
        pos = pos0 + lax.broadcasted_iota(jnp.int32, (MIX_TM, 1), 0)
        for g, w in enumerate(POOL_WINDOWS):
            cols = slice(g * POOL_GROUP, (g + 1) * POOL_GROUP)
            a_g = a_ref[:, cols]
            halo_g = halo_ref[:, cols]
            halo_g = jnp.where(pos0 == 0, jnp.zeros_like(halo_g), halo_g)
            p_main = jnp.where((d_main >= 0) & (d_main < w), 1.0, 0.0).astype(BF16)
            p_halo = jnp.where(d_halo < w, 1.0, 0.0).astype(BF16)
            wsum = (jnp.dot(p_main, a_g, preferred_element_type=F32)
                    + jnp.dot(p_halo, halo_g, preferred_element_type=F32))
            count = jnp.minimum(pos + 1, w).astype(F32)
            diff = (wsum / count - a_g.astype(F32)).astype(BF16)
            y = jnp.dot(diff, wpool_ref[g], preferred_element_type=F32) + bpool_ref[:, cols]
            pa_ref[:, cols] = (y * pscale_ref[:, cols]).astype(BF16)

        for ck in range(MIX_TM // SGU_CHUNK):
            rows = slice(ck * SGU_CHUNK, (ck + 1) * SGU_CHUNK)
            v = v_ref[rows, :].astype(F32)
            mu = jnp.mean(v, axis=-1, keepdims=True)
            vc = v - mu
            var = jnp.mean(vc * vc, axis=-1, keepdims=True)
            vn_ref[rows, :] = (vc * lax.rsqrt(var + EPS) * lng_ref[...] + lnb_ref[...]).astype(BF16)
        tr = lax.broadcasted_iota(jnp.int32, (SGU_CHUNK, SGU_CHUNK), 0)
        tc = lax.broadcasted_iota(jnp.int32, (SGU_CHUNK, SGU_CHUNK), 1)
        causal = tr >= tc
        for h in range(SGU_HEADS):
            cols = slice(h * SGU_HEAD_DIM, (h + 1) * SGU_HEAD_DIM)
            w_h = jnp.where(causal, ws_ref[h], jnp.zeros((SGU_CHUNK, SGU_CHUNK), BF16))
            bias = bst_ref[:, h:h + 1]
            for ck in range(MIX_TM // SGU_CHUNK):
                rows = slice(ck * SGU_CHUNK, (ck + 1) * SGU_CHUNK)
                z = jnp.dot(w_h, vn_ref[rows, cols], preferred_element_type=F32) + bias
                ub_ref[rows, cols] = (u_ref[rows, cols].astype(F32) * z).astype(BF16)

    y_a = jnp.dot(pa_ref[...], wbp_ref[...], preferred_element_type=F32)
    y_b = jnp.dot(ub_ref[...], wbs_ref[...], preferred_element_type=F32)
    merged = (jax.nn.sigmoid(ga_ref[...].astype(F32)) * y_a
              + jax.nn.sigmoid(gb_ref[...].astype(F32)) * y_b)
    o_ref[...] = merged.astype(o_ref.dtype)


def _mix(proj, wpool_bf, bpool, pscale, lng, lnb, ws_bf, bst, wbp_bf, wbs_bf):
    halo_blocks = MIX_TM // POOL_HALO
    const2 = lambda i, n: (0, 0)
    return pl.pallas_call(
        _mix_kernel,
        grid=(N_TOK // MIX_TM, D_MODEL // MIX_TN),
        in_specs=[
            pl.BlockSpec((MIX_TM, D_POOL), lambda i, n: (i, 0)),
            pl.BlockSpec((POOL_HALO, D_POOL), lambda i, n: (jnp.maximum(i * halo_blocks - 1, 0), 0)),
            pl.BlockSpec((MIX_TM, D_SGU), lambda i, n: (i, COL_U)),
            pl.BlockSpec((MIX_TM, D_SGU), lambda i, n: (i, COL_V)),
            pl.BlockSpec((MIX_TM, MIX_TN), lambda i, n: (i, COL_GA + n)),
            pl.BlockSpec((MIX_TM, MIX_TN), lambda i, n: (i, COL_GB + n)),
            pl.BlockSpec((len(POOL_WINDOWS), POOL_GROUP, POOL_GROUP), lambda i, n: (0, 0, 0)),
            pl.BlockSpec((1, D_POOL), const2),
            pl.BlockSpec((1, D_POOL), const2),
            pl.BlockSpec((1, D_SGU), const2),
            pl.BlockSpec((1, D_SGU), const2),
            pl.BlockSpec((SGU_HEADS, SGU_CHUNK, SGU_CHUNK), lambda i, n: (0, 0, 0)),
            pl.BlockSpec((SGU_CHUNK, SGU_HEADS), const2),
            pl.BlockSpec((D_POOL, MIX_TN), lambda i, n: (0, n)),
            pl.BlockSpec((D_SGU, MIX_TN), lambda i, n: (0, n)),
        ],
        out_specs=pl.BlockSpec((MIX_TM, MIX_TN), lambda i, n: (i, n)),
        out_shape=jax.ShapeDtypeStruct((N_TOK, D_MODEL), BF16),
        scratch_shapes=[pltpu.VMEM((MIX_TM, D_POOL), BF16),
                        pltpu.VMEM((MIX_TM, D_SGU), BF16),
                        pltpu.VMEM((MIX_TM, D_SGU), BF16)],
        compiler_params=_params(("arbitrary", "arbitrary")),
        name="mix",
    )(proj, proj, proj, proj, proj, proj, wpool_bf, bpool, pscale, lng, lnb, ws_bf, bst, wbp_bf, wbs_bf)


OUT_TM = 1024
OUT_TN = 512
SSQ_LANES = 128


def _out_kernel(m_ref, w_ref, x_ref, g_ref, x1_ref, ssq_ref):
    i = pl.program_id(0)
    j = pl.program_id(1)
    b = (i * OUT_TM) // SEQ
    acc = jnp.dot(m_ref[...], w_ref[...], preferred_element_type=F32)
    x1 = x_ref[...] + g_ref[0, pl.ds(b, 1), :] * acc
    x1_ref[...] = x1
    part = jnp.broadcast_to(jnp.sum(x1 * x1, axis=-1, keepdims=True), (OUT_TM, SSQ_LANES))

    @pl.when(j == 0)
    def _():
        ssq_ref[...] = part

    @pl.when(j > 0)
    def _():
        ssq_ref[...] += part


def _out(merged, w_out_bf, x2, mod_t):
    gate1_block = 2 * (D_MODEL // OUT_TN)
    return pl.pallas_call(
        _out_kernel,
        grid=(N_TOK // OUT_TM, D_MODEL // OUT_TN),
        in_specs=[
            pl.BlockSpec((OUT_TM, D_MODEL), lambda i, j: (i, 0)),
            pl.BlockSpec((D_MODEL, OUT_TN), lambda i, j: (0, j)),
            pl.BlockSpec((OUT_TM, OUT_TN), lambda i, j: (i, j)),
            pl.BlockSpec((1, 8, OUT_TN), lambda i, j: (gate1_block + j, 0, 0)),
        ],
        out_specs=[
            pl.BlockSpec((OUT_TM, OUT_TN), lambda i, j: (i, j)),
            pl.BlockSpec((OUT_TM, SSQ_LANES), lambda i, j: (i, 0)),
        ],
        out_shape=[jax.ShapeDtypeStruct((N_TOK, D_MODEL), F32),
                   jax.ShapeDtypeStruct((N_TOK, SSQ_LANES), F32)],
        compiler_params=_params(("arbitrary", "arbitrary")),
        name="out",
    )(merged, w_out_bf, x2, mod_t)


RT_TM = 256
NEG_INF = float("-inf")


def _first_max(v, idx, n):
    m = jnp.max(v, axis=0, keepdims=True)
    first = jnp.min(jnp.where(v == m, idx, n), axis=0, keepdims=True)
    return m, first


def _route_kernel(x1_ref, ssq_ref, sh_ref, sc_ref, wrt_ref, rb_ref, wsg_ref, wsu_ref, wsd_ref,
                  h2p_ref, shared_ref, ek_ref, posk_ref, wk_ref, cnt_ref, carry_ref):
    i = pl.program_id(0)
    b = (i * RT_TM) // SEQ

    @pl.when(i == 0)
    def _():
        carry_ref[...] = jnp.zeros_like(carry_ref)

    inv = lax.rsqrt(ssq_ref[:, 0:1] * (1.0 / D_MODEL) + EPS)
    sc = 1.0 + sc_ref[0, pl.ds(b, 1), :]
    sh = sh_ref[0, pl.ds(b, 1), :]
    h = x1_ref[...] * inv * sc + sh
    hb = h.astype(BF16)
    h2p_ref[...] = _pack_pair(h[:, :HALF], h[:, HALF:])

    g = jnp.dot(hb, wsg_ref[...], preferred_element_type=F32)
    u = jnp.dot(hb, wsu_ref[...], preferred_element_type=F32)
    act = (g * jax.nn.sigmoid(g) * u).astype(BF16)
    shared_ref[...] = jnp.dot(act, wsd_ref[...], preferred_element_type=F32).astype(shared_ref.dtype)

    logits = lax.dot_general(wrt_ref[...], hb, (((1,), (1,)), ((), ())), preferred_element_type=F32)
    scores = jax.nn.sigmoid(logits)
    biased = scores + rb_ref[:, 0:1]

    gidx = lax.broadcasted_iota(jnp.int32, (GROUP_SIZE, RT_TM), 0)
    group_vals = []
    group_score = []
    for gi in range(N_GROUPS):
        bg = biased[gi * GROUP_SIZE:(gi + 1) * GROUP_SIZE, :]
        m1, first = _first_max(bg, gidx, GROUP_SIZE)
        m2 = jnp.max(jnp.where(gidx == first, NEG_INF, bg), axis=0, keepdims=True)
        group_vals.append(bg)
        group_score.append(m1 + m2)
    masked = []
    for gi in range(N_GROUPS):
        rank = jnp.zeros((1, RT_TM), jnp.int32)
        for gj in range(N_GROUPS):
            if gj == gi:
                continue
            beats = (group_score[gj] > group_score[gi]) if gj > gi else (group_score[gj] >= group_score[gi])
            rank = rank + beats.astype(jnp.int32)
        keep = rank < TOPK_GROUPS
        masked.append(jnp.where(keep, group_vals[gi], NEG_INF))
    cand = jnp.concatenate(masked, axis=0)

    eidx = lax.broadcasted_iota(jnp.int32, (N_EXPERTS, RT_TM), 0)
    sel = jnp.zeros((N_EXPERTS, RT_TM), jnp.bool_)
    for _ in range(TOP_K):
        _, first = _first_max(cand, eidx, N_EXPERTS)
        hit = eidx == first
        sel = sel | hit
        cand = jnp.where(hit, NEG_INF, cand)

    wsel = jnp.where(sel, scores, 0.0)
    wts = wsel / jnp.sum(wsel, axis=0, keepdims=True) * ROUTED_SCALE
    self32 = jnp.where(sel, 1.0, 0.0)
    selb = self32.astype(BF16)

    tr = lax.broadcasted_iota(jnp.int32, (RT_TM, RT_TM), 0)
    tc = lax.broadcasted_iota(jnp.int32, (RT_TM, RT_TM), 1)
    before_tok = jnp.where(tr < tc, 1.0, 0.0).astype(BF16)
    er = lax.broadcasted_iota(jnp.int32, (N_EXPERTS, N_EXPERTS), 0)
    ec = lax.broadcasted_iota(jnp.int32, (N_EXPERTS, N_EXPERTS), 1)
    before_exp = jnp.where(ec < er, 1.0, 0.0).astype(BF16)
    carry = carry_ref[:, 0:1]
    pos = jnp.dot(selb, before_tok, preferred_element_type=F32) + carry
    kidx = jnp.dot(before_exp, selb, preferred_element_type=F32)
    carry_new = carry + jnp.sum(self32, axis=1, keepdims=True)
    carry_ref[...] = jnp.broadcast_to(carry_new, carry_ref.shape)
    cnt_ref[...] = jnp.broadcast_to(carry_new, cnt_ref.shape).astype(jnp.int32)

    eidx_f = eidx.astype(F32)
    for k in range(TOP_K):
        mk = sel & (kidx == float(k))
        ek_ref[k:k + 1, :] = jnp.sum(jnp.where(mk, eidx_f, 0.0), axis=0, keepdims=True).astype(jnp.int32)
        posk_ref[k:k + 1, :] = jnp.sum(jnp.where(mk, pos, 0.0), axis=0, keepdims=True).astype(jnp.int32)
        wk_ref[k:k + 1, :] = jnp.sum(jnp.where(mk, wts, 0.0), axis=0, keepdims=True)


def _route(x1, ssq, mod_r, wrt_bf, rbias, wsg_bf, wsu_bf, wsd_bf):
    const2 = lambda i: (0, 0)
    return pl.pallas_call(
        _route_kernel,
        grid=(N_TOK // RT_TM,),
        in_specs=[
            pl.BlockSpec((RT_TM, D_MODEL), lambda i: (i, 0)),
            pl.BlockSpec((RT_TM, SSQ_LANES), lambda i: (i, 0)),
            pl.BlockSpec((1, 8, D_MODEL), lambda i: (3, 0, 0)),
            pl.BlockSpec((1, 8, D_MODEL), lambda i: (4, 0, 0)),
            pl.BlockSpec((N_EXPERTS, D_MODEL), const2),
            pl.BlockSpec((N_EXPERTS, SSQ_LANES), const2),
            pl.BlockSpec((D_MODEL, D_SHARED), const2),
            pl.BlockSpec((D_MODEL, D_SHARED), const2),
            pl.BlockSpec((D_SHARED, D_MODEL), const2),
        ],
        out_specs=[
            pl.BlockSpec((RT_TM, HALF), lambda i: (i, 0)),
            pl.BlockSpec((RT_TM, D_MODEL), lambda i: (i, 0)),
            pl.BlockSpec((TOP_K, RT_TM), lambda i: (0, i)),
            pl.BlockSpec((TOP_K, RT_TM), lambda i: (0, i)),
            pl.BlockSpec((TOP_K, RT_TM), lambda i: (0, i)),
            pl.BlockSpec((N_EXPERTS, SSQ_LANES), const2),
        ],
        out_shape=[
            jax.ShapeDtypeStruct((N_TOK, HALF), jnp.uint32),
            jax.ShapeDtypeStruct((N_TOK, D_MODEL), BF16),
            jax.ShapeDtypeStruct((TOP_K, N_TOK), jnp.int32),
            jax.ShapeDtypeStruct((TOP_K, N_TOK), jnp.int32),
            jax.ShapeDtypeStruct((TOP_K, N_TOK), F32),
            jax.ShapeDtypeStruct((N_EXPERTS, SSQ_LANES), jnp.int32),
        ],
        scratch_shapes=[pltpu.VMEM((N_EXPERTS, SSQ_LANES), F32)],
        compiler_params=_params(("arbitrary",)),
        name="route",
    )(x1, ssq, mod_r, mod_r, wrt_bf, rbias, wsg_bf, wsu_bf, wsd_bf)


ISSUE_UNROLL = 8


def _expert_kernel(be_ref, nused_ref, tok_ref, tok_next_ref, slot_ref, h_hbm, wg_ref, wu_ref, wd_ref,
                   ys_hbm, xbuf, obuf, gsem, ssem):
    b = pl.program_id(0)
    n_used = nused_ref[0]
    cur = b % 2

    def gather_rows(idx_ref, buf_slot):
        def body(r, carry):
            t = idx_ref[0, 0, r]
            pltpu.make_async_copy(h_hbm.at[pl.ds(t, 1), :], xbuf.at[buf_slot, pl.ds(r, 1), :],
                                  gsem.at[buf_slot]).start()
            return carry
        lax.fori_loop(0, ROW_BLOCK, body, 0, unroll=ISSUE_UNROLL)

    def wait_gather(buf_slot):
        pltpu.make_async_copy(h_hbm.at[pl.ds(0, ROW_BLOCK), :], xbuf.at[buf_slot], gsem.at[buf_slot]).wait()

    def wait_scatter(buf_slot):
        pltpu.make_async_copy(obuf.at[buf_slot], ys_hbm.at[pl.ds(0, ROW_BLOCK), :], ssem.at[buf_slot]).wait()

    @pl.when(b == 0)
    def _():
        gather_rows(tok_ref, 0)

    @pl.when(b + 1 < n_used)
    def _():
        gather_rows(tok_next_ref, 1 - cur)

    @pl.when(b < n_used)
    def _():
        wait_gather(cur)

        @pl.when(b >= 2)
        def _():
            wait_scatter(cur)

        lo, hi = _unpack_pair(xbuf[cur])
        lo = lo.astype(BF16)
        hi = hi.astype(BF16)
        g = (jnp.dot(lo, wg_ref[0, :HALF, :], preferred_element_type=F32)
             + jnp.dot(hi, wg_ref[0, HALF:, :], preferred_element_type=F32))
        u = (jnp.dot(lo, wu_ref[0, :HALF, :], preferred_element_type=F32)
             + jnp.dot(hi, wu_ref[0, HALF:, :], preferred_element_type=F32))
        act = (g * jax.nn.sigmoid(g) * u).astype(BF16)
        y = jnp.dot(act, wd_ref[0], preferred_element_type=F32)
        obuf[cur] = _pack_pair(y[:, :HALF], y[:, HALF:])

        def body(r, carry):
            s = slot_ref[0, 0, r]
            pltpu.make_async_copy(obuf.at[cur, pl.ds(r, 1), :], ys_hbm.at[pl.ds(s, 1), :],
                                  ssem.at[cur]).start()
            return carry
        lax.fori_loop(0, ROW_BLOCK, body, 0, unroll=ISSUE_UNROLL)

        @pl.when(b == n_used - 1)
        def _():
            wait_scatter(cur)

            @pl.when(b >= 1)
            def _():
                wait_scatter(1 - cur)


def _experts(block_e, n_used, row_tok3, row_slot3, h2p, wg_bf, wu_bf, wd_bf):
    smem_block = lambda imap: pl.BlockSpec((1, 1, ROW_BLOCK), imap, memory_space=pltpu.SMEM)
    grid_spec = pltpu.PrefetchScalarGridSpec(
        num_scalar_prefetch=2,
        grid=(N_BLOCKS,),
        in_specs=[
            smem_block(lambda b, be, nu: (b, 0, 0)),
            smem_block(lambda b, be, nu: (jnp.minimum(b + 1, N_BLOCKS - 1), 0, 0)),
            smem_block(lambda b, be, nu: (b, 0, 0)),
            pl.BlockSpec(memory_space=pl.ANY),
            pl.BlockSpec((1, D_MODEL, D_EXPERT), lambda b, be, nu: (be[b], 0, 0)),
            pl.BlockSpec((1, D_MODEL, D_EXPERT), lambda b, be, nu: (be[b], 0, 0)),
            pl.BlockSpec((1, D_EXPERT, D_MODEL), lambda b, be, nu: (be[b], 0, 0)),
        ],
        out_specs=pl.BlockSpec(memory_space=pl.ANY),
        scratch_shapes=[
            pltpu.VMEM((2, ROW_BLOCK, HALF), jnp.uint32),
            pltpu.VMEM((2, ROW_BLOCK, HALF), jnp.uint32),
            pltpu.SemaphoreType.DMA((2,)),
            pltpu.SemaphoreType.DMA((2,)),
        ],
    )
    return pl.pallas_call(
        _expert_kernel,
        grid_spec=grid_spec,
        out_shape=jax.ShapeDtypeStruct((N_PAIRS + N_ROWS, HALF), jnp.uint32),
        compiler_params=_params(("arbitrary",)),
        name="experts",
    )(block_e, n_used, row_tok3, row_tok3, row_slot3, h2p, wg_bf, wu_bf, wd_bf)


CB_TM = 128


def _combine_kernel(*refs):
    y_refs = refs[:TOP_K]
    wk_ref, shared_ref, x1_ref, g_ref, gain_ref, o_ref = refs[TOP_K:]
    i = pl.program_id(0)
    b = (i * CB_TM) // SEQ
    acc_lo = jnp.zeros((CB_TM, HALF), F32)
    acc_hi = jnp.zeros((CB_TM, HALF), F32)
    for k in range(TOP_K):
        lo, hi = _unpack_pair(y_refs[k][...])
        w = wk_ref[:, k:k + 1]
        acc_lo = acc_lo + w * lo
        acc_hi = acc_hi + w * hi
    gate = g_ref[0, pl.ds(b, 1), :]
    x_lo = x1_ref[:, :HALF] + gate[:, :HALF] * (acc_lo + shared_ref[:, :HALF].astype(F32))
    x_hi = x1_ref[:, HALF:] + gate[:, HALF:] * (acc_hi + shared_ref[:, HALF:].astype(F32))
    ms = (jnp.sum(x_lo * x_lo, axis=-1, keepdims=True)
          + jnp.sum(x_hi * x_hi, axis=-1, keepdims=True)) * (1.0 / D_MODEL)
    inv = lax.rsqrt(ms + EPS)
    o_ref[:, :HALF] = x_lo * inv * gain_ref[:, :HALF]
    o_ref[:, HALF:] = x_hi * inv * gain_ref[:, HALF:]


def _combine(ys, wkt, shared, x1, mod_r, gain):
    planes = N_TOK // CB_TM
    y_specs = [pl.BlockSpec((CB_TM, HALF), functools.partial(lambda i, k: (k * planes + i, 0), k=k))
               for k in range(TOP_K)]
    return pl.pallas_call(
        _combine_kernel,
        grid=(N_TOK // CB_TM,),
        in_specs=y_specs + [
            pl.BlockSpec((CB_TM, TOP_K), lambda i: (i, 0)),
            pl.BlockSpec((CB_TM, D_MODEL), lambda i: (i, 0)),
            pl.BlockSpec((CB_TM, D_MODEL), lambda i: (i, 0)),
            pl.BlockSpec((1, 8, D_MODEL), lambda i: (5, 0, 0)),
            pl.BlockSpec((1, D_MODEL), lambda i: (0, 0)),
        ],
        out_specs=pl.BlockSpec((CB_TM, D_MODEL), lambda i: (i, 0)),
        out_shape=jax.ShapeDtypeStruct((N_TOK, D_MODEL), F32),
        compiler_params=_params(("arbitrary",)),
        name="combine",
    )(*([ys] * TOP_K), wkt, shared, x1, mod_r, gain)


def _dispatch_plan(ek, posk, counts):
    padded = (counts + ROW_BLOCK - 1) // ROW_BLOCK * ROW_BLOCK
    pends = jnp.cumsum(padded)
    pstarts = pends - padded
    dest = (pstarts[ek] + posk).reshape(N_PAIRS)
    pair = jnp.full((N_ROWS,), -1, jnp.int32).at[dest].set(jnp.arange(N_PAIRS, dtype=jnp.int32))
    valid = pair >= 0
    row_tok = jnp.where(valid, pair % N_TOK, 0)
    row_slot = jnp.where(valid, pair, N_PAIRS + jnp.arange(N_ROWS, dtype=jnp.int32))
    block_start = jnp.arange(N_BLOCKS, dtype=jnp.int32) * ROW_BLOCK
    block_e = jnp.minimum(jnp.searchsorted(pends, block_start, side="right"), N_EXPERTS - 1)
    n_used = (pends[-1] // ROW_BLOCK).reshape(1)
    return (block_e.astype(jnp.int32), n_used.astype(jnp.int32),
            row_tok.reshape(N_BLOCKS, 1, ROW_BLOCK), row_slot.reshape(N_BLOCKS, 1, ROW_BLOCK))


def kernel(x, c, w_ada, b_ada, w_in, w_pool, b_pool, pool_scale, sgu_ln_g, sgu_ln_b, w_spatial,
           b_spatial, w_branch_pool, w_branch_sgu, w_out, w_router, router_bias, w_exp_gate,
           w_exp_up, w_exp_down, w_sh_gate, w_sh_up, w_sh_down, final_gain):
    assert x.shape == (BATCH, SEQ, D_MODEL) and w_ada.shape[0] == 1
    x2 = x.reshape(N_TOK, D_MODEL)
    c8 = jnp.concatenate([c, jnp.zeros((8 - BATCH, D_MODEL), c.dtype)], axis=0)

    mod = _ada(c8, w_ada[0], b_ada[0].reshape(1, N_MOD * D_MODEL))
    mod_r = mod.reshape(8, N_MOD, D_MODEL).transpose(1, 0, 2)
    mod_t = mod.reshape(8, N_MOD * D_MODEL // OUT_TN, OUT_TN).transpose(1, 0, 2)

    proj = _proj(x2, mod_r, w_in[0].astype(BF16))
    merged = _mix(
        proj, w_pool[0].astype(BF16), b_pool[0].reshape(1, D_POOL), pool_scale[0].reshape(1, D_POOL),
        sgu_ln_g[0].reshape(1, D_SGU), sgu_ln_b[0].reshape(1, D_SGU), w_spatial[0].astype(BF16),
        b_spatial[0].T, w_branch_pool[0].astype(BF16), w_branch_sgu[0].astype(BF16))
    x1, ssq = _out(merged, w_out[0].astype(BF16), x2, mod_t)

    rbias = jnp.broadcast_to(router_bias[0].reshape(N_EXPERTS, 1), (N_EXPERTS, SSQ_LANES))
    h2p, shared, ek, posk, wk, cnt = _route(
        x1, ssq, mod_r, w_router[0].T.astype(BF16), rbias,
        w_sh_gate[0].astype(BF16), w_sh_up[0].astype(BF16), w_sh_down[0].astype(BF16))

    block_e, n_used, row_tok3, row_slot3 = _dispatch_plan(ek, posk, cnt[:, 0])
    ys = _experts(block_e, n_used, row_tok3, row_slot3, h2p,
                  w_exp_gate[0].astype(BF16), w_exp_up[0].astype(BF16), w_exp_down[0].astype(BF16))
    out = _combine(ys, wk.T, shared, x1, mod_r, final_gain.reshape(1, D_MODEL))
    return out.reshape(BATCH, SEQ, D_MODEL)
```
